```python
import jax, jax.numpy as jnp
from jax import lax
import numpy as np

D_MODEL = 1024
BATCH = 16
SEQ = 2048
DEPTH = 2

D_FOURIER = D_MODEL // 2
FOURIER_GROUP = 64
N_FOURIER_GROUPS = D_FOURIER // FOURIER_GROUP
D_SCONV = D_MODEL // 2
SCONV_WIDTH = 3
D_CONF = D_MODEL // 2
CONF_WIDTH = 31
N_BRANCH = 3
D_IN = D_FOURIER + 3 * D_SCONV + 2 * D_CONF
D_FF = ((8 * D_MODEL // 3 + 127) // 128) * 128
N_MOD = 9
EPS = 1e-6

kernel_name = "hybrid_fourier_shortconv_conformer_macaron_adaln"


def rms_norm(x, g):
    xf = x.astype(jnp.float32)
    y = xf * lax.rsqrt(jnp.mean(xf * xf, axis=-1, keepdims=True) + EPS)
    return (y * g.astype(jnp.float32)).astype(x.dtype)


def layer_norm(x, g, b):
    xf = x.astype(jnp.float32)
    mu = jnp.mean(xf, axis=-1, keepdims=True)
    var = jnp.mean(jnp.square(xf - mu), axis=-1, keepdims=True)
    y = (xf - mu) * lax.rsqrt(var + EPS)
    return (y * g.astype(jnp.float32) + b.astype(jnp.float32)).astype(x.dtype)


def modulate(h, shift, scale):
    return h * (1.0 + scale[:, None, :]) + shift[:, None, :]


def swiglu(h, w_gate, w_up, w_down):
    return (jax.nn.silu(h @ w_gate) * (h @ w_up)) @ w_down


def depthwise_conv(x, w):
    return lax.conv_general_dilated(
        x, w[:, None, :], window_strides=(1,), padding="SAME",
        dimension_numbers=("NWC", "WIO", "NWC"), feature_group_count=x.shape[-1])


def fourier_mix(u):
    b, s, _ = u.shape
    ug = u.astype(jnp.float32).reshape(b, s, N_FOURIER_GROUPS, FOURIER_GROUP)
    f = jnp.fft.fft2(ug, axes=(1, 3), norm="ortho").real
    return f.reshape(b, s, D_FOURIER).astype(u.dtype)


def token_mixing(h, w_in, conv_short_w, conv_conf_w, conv_conf_b, conf_ln_g, conf_ln_b,
                 w_branch_f, w_branch_s, w_branch_c, w_gate, b_gate, w_out):
    b, s, d = h.shape
    u = h @ w_in
    cuts = np.cumsum([D_FOURIER, D_SCONV, D_SCONV, D_SCONV, D_CONF]).tolist()
    u_f, u_bg, u_cg, u_x, u_ga, u_gb = jnp.split(u, cuts, axis=-1)
    y_f = fourier_mix(u_f) @ w_branch_f
    y_s = (u_bg * depthwise_conv(u_cg * u_x, conv_short_w)) @ w_branch_s
    v = u_ga * jax.nn.sigmoid(u_gb)
    v = depthwise_conv(v, conv_conf_w) + conv_conf_b
    v = jax.nn.silu(layer_norm(v, conf_ln_g, conf_ln_b))
    y_c = v @ w_branch_c
    g = jax.nn.sigmoid(h @ w_gate + b_gate).reshape(b, s, N_BRANCH, d)
    merged = g[:, :, 0] * y_f + g[:, :, 1] * y_s + g[:, :, 2] * y_c
    return merged @ w_out


def setup_inputs(seed: int = 0) -> dict:
    key = jax.random.key(seed)
    ks = iter(jax.random.split(key, 40))

    def nrm(shape, scale):
        return jax.random.normal(next(ks), shape, jnp.float32) * scale

    def gain(shape):
        return 1.0 + nrm(shape, 0.02)

    L, D = DEPTH, D_MODEL
    return {
        "x": nrm((BATCH, SEQ, D), 1.0),
        "c": nrm((BATCH, D), 1.0),
        "ffn1_norm_g": gain((L, D)),
        "ffn1_w_gate": nrm((L, D, D_FF), D ** -0.5),
        "ffn1_w_up": nrm((L, D, D_FF), D ** -0.5),
        "ffn1_w_down": nrm((L, D_FF, D), D_FF ** -0.5),
        "mix_norm_g": gain((L, D)),
        "w_in": nrm((L, D, D_IN), D ** -0.5),
        "conv_short_w": nrm((L, SCONV_WIDTH, D_SCONV), SCONV_WIDTH ** -0.5),
        "conv_conf_w": nrm((L, CONF_WIDTH, D_CONF), CONF_WIDTH ** -0.5),
        "conv_conf_b": nrm((L, D_CONF), 0.02),
        "conf_ln_g": gain((L, D_CONF)),
        "conf_ln_b": nrm((L, D_CONF), 0.02),
        "w_branch_f": nrm((L, D_FOURIER, D), D_FOURIER ** -0.5),
        "w_branch_s": nrm((L, D_SCONV, D), D_SCONV ** -0.5),
        "w_branch_c": nrm((L, D_CONF, D), D_CONF ** -0.5),
        "w_gate": nrm((L, D, N_BRANCH * D), D ** -0.5),
        "b_gate": nrm((L, N_BRANCH * D), 0.02),
        "w_out": nrm((L, D, D), D ** -0.5),
        "ffn2_norm_g": gain((L, D)),
        "ffn2_w_gate": nrm((L, D, D_FF), D ** -0.5),
        "ffn2_w_up": nrm((L, D, D_FF), D ** -0.5),
        "ffn2_w_down": nrm((L, D_FF, D), D_FF ** -0.5),
        "w_mod": nrm((L, D, N_MOD * D), 0.5 * D ** -0.5),
        "b_mod": nrm((L, N_MOD * D), 0.02),
        "final_norm_g": gain((D,)),
        "w_final_mod": nrm((D, 2 * D), 0.5 * D ** -0.5),
        "b_final_mod": nrm((2 * D,), 0.02),
    }


def reference(x, c, ffn1_norm_g, ffn1_w_gate, ffn1_w_up, ffn1_w_down, mix_norm_g, w_in,
              conv_short_w, conv_conf_w, conv_conf_b, conf_ln_g, conf_ln_b,
              w_branch_f, w_branch_s, w_branch_c, w_gate, b_gate, w_out,
              ffn2_norm_g, ffn2_w_gate, ffn2_w_up, ffn2_w_down, w_mod, b_mod,
              final_norm_g, w_final_mod, b_final_mod):
    c_act = jax.nn.silu(c)
    for l in range(DEPTH):
        mod = c_act @ w_mod[l] + b_mod[l]
        (sh1, sc1, g1, sh2, sc2, g2, sh3, sc3, g3) = jnp.split(mod, N_MOD, axis=-1)
        h = modulate(rms_norm(x, ffn1_norm_g[l]), sh1, sc1)
        x = x + 0.5 * g1[:, None, :] * swiglu(h, ffn1_w_gate[l], ffn1_w_up[l], ffn1_w_down[l])
        h = modulate(rms_norm(x, mix_norm_g[l]), sh2, sc2)
        y = token_mixing(h, w_in[l], conv_short_w[l], conv_conf_w[l], conv_conf_b[l],
                         conf_ln_g[l], conf_ln_b[l], w_branch_f[l], w_branch_s[l],
                         w_branch_c[l], w_gate[l], b_gate[l], w_out[l])
        x = x + g2[:, None, :] * y
        h = modulate(rms_norm(x, ffn2_norm_g[l]), sh3, sc3)
        x = x + 0.5 * g3[:, None, :] * swiglu(h, ffn2_w_gate[l], ffn2_w_up[l], ffn2_w_down[l])
    fmod = c_act @ w_final_mod + b_final_mod
    f_shift, f_scale = jnp.split(fmod, 2, axis=-1)
    return modulate(rms_norm(x, final_norm_g), f_shift, f_scale)
```

```python
import functools

import numpy as np
import jax
import jax.numpy as jnp
from jax import lax
from jax.experimental import pallas as pl
from jax.experimental.pallas import tpu as pltpu

EPS = 1e-6
FOURIER_GROUP = 64
N_MOD = 9
BF16 = jnp.bfloat16
F32 = jnp.float32

VMEM_LIMIT_BYTES_V7X = 56 * 1024 * 1024
CONV_PAD_ROWS = 16
SUBLANES, LANES = 8, 128


def _compiler_params(n_grid_dims):
    return pltpu.CompilerParams(
        dimension_semantics=("arbitrary",) * n_grid_dims,
        vmem_limit_bytes=VMEM_LIMIT_BYTES_V7X)


def _resident(block_shape, index_map):
    return pl.BlockSpec(block_shape, index_map, pipeline_mode=pl.Buffered(1))


def _sigmoid(v):
    return 1.0 / (1.0 + jnp.exp(-v))


def _norm_modulate(x, gain, shift, scale):
    ms = jnp.mean(x * x, axis=-1, keepdims=True)
    y = x * lax.rsqrt(ms + EPS) * gain
    return y * (1.0 + scale) + shift


def _mod_kernel(c_ref, w_ref, b_ref, o_ref):
    c = c_ref[...]
    ca = (c * _sigmoid(c)).astype(BF16)
    o_ref[...] = jnp.dot(ca, w_ref[...].astype(BF16), preferred_element_type=F32) + b_ref[...]


def _mod_call(c, w, b, tn=1024):
    g, d, n = w.shape
    bsz = c.shape[0]
    return pl.pallas_call(
        _mod_kernel,
        grid=(g, n // tn),
        in_specs=[
            pl.BlockSpec((bsz, d), lambda i, j: (0, 0)),
            pl.BlockSpec((None, d, tn), lambda i, j: (i, 0, j)),
            pl.BlockSpec((None, 1, tn), lambda i, j: (i, 0, j)),
        ],
        out_specs=pl.BlockSpec((None, bsz, tn), lambda i, j: (i, 0, j)),
        out_shape=jax.ShapeDtypeStruct((g, bsz, n), F32),
        compiler_params=_compiler_params(2),
        name="adaln_mod",
    )(c, w, b)


def _fold_kernel(w_ref, m_ref, o_ref):
    o_ref[...] = jnp.dot(w_ref[...], m_ref[...], preferred_element_type=F32,
                         precision=lax.Precision.HIGHEST)


def _fold_call(w_in, chan_dft, d_f):
    nl, d, _ = w_in.shape
    return pl.pallas_call(
        _fold_kernel,
        grid=(nl,),
        in_specs=[
            pl.BlockSpec((None, d, d_f), lambda l: (l, 0, 0)),
            pl.BlockSpec((d_f, 2 * d_f), lambda l: (0, 0)),
        ],
        out_specs=pl.BlockSpec((None, d, 2 * d_f), lambda l: (l, 0, 0)),
        out_shape=jax.ShapeDtypeStruct((nl, d, 2 * d_f), F32),
        compiler_params=_compiler_params(1),
        name="fold_channel_dft",
    )(w_in, chan_dft)


def _ffn_kernel(*refs, tf, final):
    if final:
        (x_ref, sh_ref, sc_ref, gt_ref, ng_ref, wg_ref, wu_ref, wd_ref,
         fg_ref, fsh_ref, fsc_ref, o_ref) = refs
    else:
        x_ref, sh_ref, sc_ref, gt_ref, ng_ref, wg_ref, wu_ref, wd_ref, o_ref = refs
    x = x_ref[...]
    h = _norm_modulate(x, ng_ref[...], sh_ref[...], sc_ref[...]).astype(BF16)
    d_ff = wg_ref.shape[1]
    acc = None
    for j in range(d_ff // tf):
        cols = slice(j * tf, (j + 1) * tf)
        g = jnp.dot(h, wg_ref[:, cols], preferred_element_type=F32)
        u = jnp.dot(h, wu_ref[:, cols], preferred_element_type=F32)
        a = (g * _sigmoid(g) * u).astype(BF16)
        part = jnp.dot(a, wd_ref[cols, :], preferred_element_type=F32)
        acc = part if acc is None else acc + part
    y = x + (0.5 * gt_ref[...]) * acc
    if final:
        y = _norm_modulate(y, fg_ref[...], fsh_ref[...], fsc_ref[...])
    o_ref[...] = y


def _ffn_call(x2, mod3, sub, norm_g, wg, wu, wd, seq, tm, tf, final_args=None):
    n, d = x2.shape
    d_ff = wg.shape[1]
    final = final_args is not None
    row_spec = pl.BlockSpec((tm, d), lambda i: (i, 0))

    def mod_spec(col):
        return pl.BlockSpec((None, 1, d), lambda i: (i * tm // seq, 0, col))

    in_specs = [
        row_spec, mod_spec(3 * sub), mod_spec(3 * sub + 1), mod_spec(3 * sub + 2),
        pl.BlockSpec((1, d), lambda i: (0, 0)),
        _resident((d, d_ff), lambda i: (0, 0)),
        _resident((d, d_ff), lambda i: (0, 0)),
        _resident((d_ff, d), lambda i: (0, 0)),
    ]
    args = [x2, mod3, mod3, mod3, norm_g, wg, wu, wd]
    if final:
        fg, fmod3 = final_args
        in_specs += [
            pl.BlockSpec((1, d), lambda i: (0, 0)),
            pl.BlockSpec((None, 1, d), lambda i: (i * tm // seq, 0, 0)),
            pl.BlockSpec((None, 1, d), lambda i: (i * tm // seq, 0, 1)),
        ]
        args += [fg, fmod3, fmod3]
    return pl.pallas_call(
        functools.partial(_ffn_kernel, tf=tf, final=final),
        grid=(n // tm,),
        in_specs=in_specs,
        out_specs=row_spec,
        out_shape=jax.ShapeDtypeStruct((n, d), F32),
        compiler_params=_compiler_params(1),
        name="ffn_final" if final else "ffn",
    )(*args)


def _mix_in_kernel(x_ref, sh_ref, sc_ref, ng_ref, w_ref, o_ref, *, dc):
    x = x_ref[...]
    h = _norm_modulate(x, ng_ref[...], sh_ref[...], sc_ref[...]).astype(BF16)
    uf = jnp.dot(h, w_ref[:, 0:2 * dc], preferred_element_type=F32)
    o_ref[:, 0:2 * dc] = uf.astype(BF16)
    us = jnp.dot(h, w_ref[:, 2 * dc:5 * dc], preferred_element_type=F32)
    o_ref[:, 2 * dc:3 * dc] = us[:, 0:dc].astype(BF16)
    o_ref[:, 3 * dc:4 * dc] = (us[:, dc:2 * dc] * us[:, 2 * dc:3 * dc]).astype(BF16)
    ug = jnp.dot(h, w_ref[:, 5 * dc:7 * dc], preferred_element_type=F32)
    o_ref[:, 4 * dc:5 * dc] = (ug[:, 0:dc] * _sigmoid(ug[:, dc:2 * dc])).astype(BF16)


def _mix_in_call(x2, mod3, norm_g, w, seq, tm, dc):
    n, d = x2.shape
    return pl.pallas_call(
        functools.partial(_mix_in_kernel, dc=dc),
        grid=(n // tm,),
        in_specs=[
            pl.BlockSpec((tm, d), lambda i: (i, 0)),
            pl.BlockSpec((None, 1, d), lambda i: (i * tm // seq, 0, 3)),
            pl.BlockSpec((None, 1, d), lambda i: (i * tm // seq, 0, 4)),
            pl.BlockSpec((1, d), lambda i: (0, 0)),
            _resident(w.shape, lambda i: (0, 0)),
        ],
        out_specs=pl.BlockSpec((tm, 5 * dc), lambda i: (i, 0)),
        out_shape=jax.ShapeDtypeStruct((n, 5 * dc), BF16),
        compiler_params=_compiler_params(1),
        name="mix_in",
    )(x2, mod3, mod3, norm_g, w)


def _fill_padded(dst_ref, src_ref, t0, tr, seq, k, nk):
    pad = CONV_PAD_ROWS
    dst_ref[pad:pad + tr, :] = src_ref[pl.ds(t0, tr), :].astype(F32)
    top = src_ref[pl.ds(pl.multiple_of(jnp.maximum(t0 - pad, 0), pad), pad), :].astype(F32)
    dst_ref[0:pad, :] = jnp.where(k > 0, top, 0.0)
    bot = src_ref[pl.ds(pl.multiple_of(jnp.minimum(t0 + tr, seq - pad), pad), pad), :].astype(F32)
    dst_ref[pad + tr:pad + tr + pad, :] = jnp.where(k < nk - 1, bot, 0.0)


def _depthwise_rows(pad_ref, w_ref, base, rows, width):
    half = (width - 1) // 2
    win_rows = rows + 2 * CONV_PAD_ROWS
    blocks = []
    for cb in range(pad_ref.shape[1] // LANES):
        lanes = slice(cb * LANES, (cb + 1) * LANES)
        win = pad_ref[pl.ds(base, win_rows), lanes]
        acc = None
        for r in range(SUBLANES):
            offs = [o for o in range(CONV_PAD_ROWS - half, CONV_PAD_ROWS + half + 1)
                    if o % SUBLANES == r]
            if not offs:
                continue
            shifted = win if r == 0 else pltpu.roll(win, win_rows - r, 0)
            for o in offs:
                tap = o - (CONV_PAD_ROWS - half)
                term = shifted[o - r:o - r + rows, :] * w_ref[tap:tap + 1, lanes]
                acc = term if acc is None else acc + term
        blocks.append(acc)
    return jnp.concatenate(blocks, axis=-1)


def _mix_seq_kernel(uf_ref, bg_ref, p_ref, v_ref, cm_ref, sm_ref, w3_ref, w31_ref, cb_ref,
                    lg_ref, lb_ref, o_ref, ppad_ref, vpad_ref, *, tr, seq, dc, rows):
    k = pl.program_id(1)
    nk = pl.num_programs(1)
    t0 = pl.multiple_of(k * tr, tr)

    zf = jnp.dot(cm_ref[...], uf_ref[:, 0:dc], preferred_element_type=F32)
    zf = zf + jnp.dot(sm_ref[...], uf_ref[:, dc:2 * dc], preferred_element_type=F32)
    o_ref[:, 0:dc] = zf.astype(BF16)

    _fill_padded(ppad_ref, p_ref, t0, tr, seq, k, nk)
    _fill_padded(vpad_ref, v_ref, t0, tr, seq, k, nk)

    def body(i, carry):
        base = pl.multiple_of(i * rows, rows)
        cs = _depthwise_rows(ppad_ref, w3_ref, base, rows, w3_ref.shape[0])
        zs = bg_ref[pl.ds(base, rows), :].astype(F32) * cs
        o_ref[pl.ds(base, rows), dc:2 * dc] = zs.astype(BF16)
        cv = _depthwise_rows(vpad_ref, w31_ref, base, rows, w31_ref.shape[0]) + cb_ref[...]
        mu = jnp.mean(cv, axis=-1, keepdims=True)
        cen = cv - mu
        var = jnp.mean(cen * cen, axis=-1, keepdims=True)
        ln = cen * lax.rsqrt(var + EPS) * lg_ref[...] + lb_ref[...]
        o_ref[pl.ds(base, rows), 2 * dc:3 * dc] = (ln * _sigmoid(ln)).astype(BF16)
        return carry

    lax.fori_loop(0, tr // rows, body, 0)


def _mix_seq_call(u3, cmat, smat, w3, w31, cb, lg, lb, tr, dc, rows):
    bsz, seq, _ = u3.shape
    small = lambda shape: pl.BlockSpec(shape, lambda b, k: (0, 0))
    return pl.pallas_call(
        functools.partial(_mix_seq_kernel, tr=tr, seq=seq, dc=dc, rows=rows),
        grid=(bsz, seq // tr),
        in_specs=[
            pl.BlockSpec((None, seq, 2 * dc), lambda b, k: (b, 0, 0)),
            pl.BlockSpec((None, tr, dc), lambda b, k: (b, k, 2)),
            pl.BlockSpec((None, seq, dc), lambda b, k: (b, 0, 3)),
            pl.BlockSpec((None, seq, dc), lambda b, k: (b, 0, 4)),
            pl.BlockSpec((tr, seq), lambda b, k: (k, 0)),
            pl.BlockSpec((tr, seq), lambda b, k: (k, 0)),
            small(w3.shape), small(w31.shape), small(cb.shape), small(lg.shape), small(lb.shape),
        ],
        out_specs=pl.BlockSpec((None, tr, 3 * dc), lambda b, k: (b, k, 0)),
        out_shape=jax.ShapeDtypeStruct((bsz, seq, 3 * dc), BF16),
        scratch_shapes=[
            pltpu.VMEM((tr + 2 * CONV_PAD_ROWS, dc), F32),
            pltpu.VMEM((tr + 2 * CONV_PAD_ROWS, dc), F32),
        ],
        compiler_params=_compiler_params(2),
        name="mix_seq",
    )(u3, u3, u3, u3, cmat, smat, w3, w31, cb, lg, lb)


def _mix_out_kernel(x_ref, z_ref, sh_ref, sc_ref, gt_ref, ng_ref, wgate_ref, bgate_ref,
                    wbf_ref, wbs_ref, wbc_ref, wo_ref, o_ref, *, dc):
    x = x_ref[...]
    d = x.shape[1]
    h = _norm_modulate(x, ng_ref[...], sh_ref[...], sc_ref[...]).astype(BF16)
    merged = None
    for i, wb_ref in enumerate((wbf_ref, wbs_ref, wbc_ref)):
        logits = jnp.dot(h, wgate_ref[:, i * d:(i + 1) * d], preferred_element_type=F32)
        gate = _sigmoid(logits + bgate_ref[:, i * d:(i + 1) * d])
        y = jnp.dot(z_ref[:, i * dc:(i + 1) * dc], wb_ref[...], preferred_element_type=F32)
        merged = gate * y if merged is None else merged + gate * y
    out = jnp.dot(merged.astype(BF16), wo_ref[...], preferred_element_type=F32)
    o_ref[...] = x + gt_ref[...] * out


def _mix_out_call(x2, z2, mod3, norm_g, wgate, bgate, wbf, wbs, wbc, wo, seq, tm, dc):
    n, d = x2.shape

    def mod_spec(col):
        return pl.BlockSpec((None, 1, d), lambda i: (i * tm // seq, 0, col))

    const = lambda a: _resident(a.shape, lambda i: (0, 0))
    return pl.pallas_call(
        functools.partial(_mix_out_kernel, dc=dc),
        grid=(n // tm,),
        in_specs=[
            pl.BlockSpec((tm, d), lambda i: (i, 0)),
            pl.BlockSpec((tm, 3 * dc), lambda i: (i, 0)),
            mod_spec(3), mod_spec(4), mod_spec(5),
            pl.BlockSpec((1, d), lambda i: (0, 0)),
            const(wgate), pl.BlockSpec(bgate.shape, lambda i: (0, 0)),
            const(wbf), const(wbs), const(wbc), const(wo),
        ],
        out_specs=pl.BlockSpec((tm, d), lambda i: (i, 0)),
        out_shape=jax.ShapeDtypeStruct((n, d), F32),
        compiler_params=_compiler_params(1),
        name="mix_out",
    )(x2, z2, mod3, mod3, mod3, norm_g, wgate, bgate, wbf, wbs, wbc, wo)


@functools.lru_cache(maxsize=None)
def _seq_dft_matrices(seq):
    idx = np.arange(seq, dtype=np.int64)
    ang = (2.0 * np.pi / seq) * ((idx[:, None] * idx[None, :]) % seq).astype(np.float64)
    return (np.cos(ang).astype(np.float32).astype(jnp.bfloat16),
            (-np.sin(ang)).astype(np.float32).astype(jnp.bfloat16))


@functools.lru_cache(maxsize=None)
def _chan_dft_matrix(d_f, group, seq):
    idx = np.arange(group, dtype=np.int64)
    ang = (2.0 * np.pi / group) * ((idx[:, None] * idx[None, :]) % group).astype(np.float64)
    scale = 1.0 / np.sqrt(float(seq) * float(group))
    out = np.zeros((d_f, 2 * d_f), np.float64)
    for g in range(d_f // group):
        sl = slice(g * group, (g + 1) * group)
        out[sl, sl] = np.cos(ang) * scale
        out[sl, d_f + g * group:d_f + (g + 1) * group] = np.sin(ang) * scale
    return out.astype(np.float32)


def kernel(x, c, ffn1_norm_g, ffn1_w_gate, ffn1_w_up, ffn1_w_down, mix_norm_g, w_in, conv_short_w, conv_conf_w, conv_conf_b, conf_ln_g, conf_ln_b, w_branch_f, w_branch_s, w_branch_c, w_gate, b_gate, w_out, ffn2_norm_g, ffn2_w_gate, ffn2_w_up, ffn2_w_down, w_mod, b_mod, final_norm_g, w_final_mod, b_final_mod):
    bsz, seq, d = x.shape
    depth = w_mod.shape[0]
    dc = w_branch_f.shape[1]
    n = bsz * seq
    tm_ffn, tf = 512, 1408
    tm_mix = 512
    tr, conv_rows = 512, 64

    mods = _mod_call(c, w_mod, b_mod.reshape(depth, 1, N_MOD * d))
    fmod = _mod_call(c, w_final_mod[None], b_final_mod.reshape(1, 1, 2 * d))
    fmod3 = fmod.reshape(bsz, 1, 2 * d)

    cmat, smat = _seq_dft_matrices(seq)
    cmat, smat = jnp.asarray(cmat), jnp.asarray(smat)
    folded = _fold_call(w_in, jnp.asarray(_chan_dft_matrix(dc, FOURIER_GROUP, seq)), dc)
    w_in_all = jnp.concatenate([folded, w_in[:, :, dc:]], axis=-1).astype(BF16)

    to_bf16 = lambda a: a.astype(BF16)
    f1g, f1u, f1d = to_bf16(ffn1_w_gate), to_bf16(ffn1_w_up), to_bf16(ffn1_w_down)
    f2g, f2u, f2d = to_bf16(ffn2_w_gate), to_bf16(ffn2_w_up), to_bf16(ffn2_w_down)
    wgate_b, wo_b = to_bf16(w_gate), to_bf16(w_out)
    wbf, wbs, wbc = to_bf16(w_branch_f), to_bf16(w_branch_s), to_bf16(w_branch_c)

    x2 = x.reshape(n, d)
    for l in range(depth):
        mod3 = mods[l].reshape(bsz, 1, N_MOD * d)
        x2 = _ffn_call(x2, mod3, 0, ffn1_norm_g[l][None], f1g[l], f1u[l], f1d[l], seq, tm_ffn, tf)
        u2 = _mix_in_call(x2, mod3, mix_norm_g[l][None], w_in_all[l], seq, tm_mix, dc)
        z3 = _mix_seq_call(u2.reshape(bsz, seq, 5 * dc), cmat, smat, conv_short_w[l],
                           conv_conf_w[l], conv_conf_b[l][None], conf_ln_g[l][None],
                           conf_ln_b[l][None], tr, dc, conv_rows)
        x2 = _mix_out_call(x2, z3.reshape(n, 3 * dc), mod3, mix_norm_g[l][None], wgate_b[l],
                           b_gate[l][None], wbf[l], wbs[l], wbc[l], wo_b[l], seq, tm_mix, dc)
        final_args = (final_norm_g[None], fmod3) if l == depth - 1 else None
        x2 = _ffn_call(x2, mod3, 2, ffn2_norm_g[l][None], f2g[l], f2u[l], f2d[l], seq, tm_ffn, tf,
                       final_args=final_args)
    return x2.reshape(bsz, seq, d)
```

```python
import functools

import numpy as np
import jax
import jax.numpy as jnp
from jax import lax
from jax.experimental import pallas as pl
from jax.experimental.pallas import tpu as pltpu

EPS = 1e-6
FOURIER_GROUP = 64
N_MOD = 9
BF16 = jnp.bfloat16
F32 = jnp.float32

VMEM_LIMIT_BYTES_V7X = 56 * 1024 * 1024
CONV_PAD_ROWS = 16
SUBLANES, LANES = 8, 128


def _compiler_params(n_grid_dims):
    return pltpu.CompilerParams(
        dimension_semantics=("arbitrary",) * n_grid_dims,
        vmem_limit_bytes=VMEM_LIMIT_BYTES_V7X)


def _resident(block_shape, index_map):
    return pl.BlockSpec(block_shape, index_map, pipeline_mode=pl.Buffered(1))


def _sigmoid(v):
    return 1.0 / (1.0 + jnp.exp(-v))


def _norm_modulate(x, gain, shift, scale):
    ms = jnp.mean(x * x, axis=-1, keepdims=True)
    y = x * lax.rsqrt(ms + EPS) * gain
    return y * (1.0 + scale) + shift


def _mod_kernel(c_ref, w_ref, b_ref, o_ref):
    c = c_ref[...]
    ca = (c * _sigmoid(c)).astype(BF16)
    o_ref[...] = jnp.dot(ca, w_ref[...].astype(BF16), preferred_element_type=F32) + b_ref[...]


def _mod_call(c, w, b, tn=1024):
    g, d, n = w.shape
    bsz = c.shape[0]
    return pl.pallas_call(
        _mod_kernel,
        grid=(g, n // tn),
        in_specs=[
            pl.BlockSpec((bsz, d), lambda i, j: (0, 0)),
            pl.BlockSpec((None, d, tn), lambda i, j: (i, 0, j)),
            pl.BlockSpec((None, 1, tn), lambda i, j: (i, 0, j)),
        ],
        out_specs=pl.BlockSpec((None, bsz, tn), lambda i, j: (i, 0, j)),
        out_shape=jax.ShapeDtypeStruct((g, bsz, n), F32),
        compiler_params=_compiler_params(2),
        name="adaln_mod",
    )(c, w, b)


def _fold_kernel(w_ref, m_ref, o_ref):
    o_ref[...] = jnp.dot(w_ref[...], m_ref[...], preferred_element_type=F32,
                         precision=lax.Precision.HIGHEST)


def _fold_call(w_in, chan_dft, d_f):
    nl, d, _ = w_in.shape
    return pl.pallas_call(
        _fold_kernel,
        grid=(nl,),
        in_specs=[
            pl.BlockSpec((None, d, d_f), lambda l: (l, 0, 0)),
            pl.BlockSpec((d_f, 2 * d_f), lambda l: (0, 0)),
        ],
        out_specs=pl.BlockSpec((None, d, 2 * d_f), lambda l: (l, 0, 0)),
        out_shape=jax.ShapeDtypeStruct((nl, d, 2 * d_f), F32),
        compiler_params=_compiler_params(1),
        name="fold_channel_dft",
    )(w_in, chan_dft)


def _ffn_kernel(*refs, chunks, final):
    if final:
        (x_ref, sh_ref, sc_ref, gt_ref, ng_ref, wg_ref, wu_ref, wd_ref,
         fg_ref, fsh_ref, fsc_ref, o_ref) = refs
    else:
        x_ref, sh_ref, sc_ref, gt_ref, ng_ref, wg_ref, wu_ref, wd_ref, o_ref = refs
    x = x_ref[...]
    h = _norm_modulate(x, ng_ref[...], sh_ref[...], sc_ref[...]).astype(BF16)
    acc = None
    col = 0
    for width in chunks:
        cols = slice(col, col + width)
        col += width
        g = jnp.dot(h, wg_ref[:, cols], preferred_element_type=F32)
        u = jnp.dot(h, wu_ref[:, cols], preferred_element_type=F32)
        a = (g * _sigmoid(g) * u).astype(BF16)
        part = jnp.dot(a, wd_ref[cols, :], preferred_element_type=F32)
        acc = part if acc is None else acc + part
    y = x + (0.5 * gt_ref[...]) * acc
    if final:
        y = _norm_modulate(y, fg_ref[...], fsh_ref[...], fsc_ref[...])
    o_ref[...] = y


def _ffn_call(x2, mod3, sub, norm_g, wg, wu, wd, seq, tm, chunks, final_args=None):
    n, d = x2.shape
    d_ff = wg.shape[1]
    assert sum(chunks) == d_ff
    final = final_args is not None
    row_spec = pl.BlockSpec((tm, d), lambda i: (i, 0))

    def mod_spec(col):
        return pl.BlockSpec((None, 1, d), lambda i: (i * tm // seq, 0, col))

    in_specs = [
        row_spec, mod_spec(3 * sub), mod_spec(3 * sub + 1), mod_spec(3 * sub + 2),
        pl.BlockSpec((1, d), lambda i: (0, 0)),
        _resident((d, d_ff), lambda i: (0, 0)),
        _resident((d, d_ff), lambda i: (0, 0)),
        _resident((d_ff, d), lambda i: (0, 0)),
    ]
    args = [x2, mod3, mod3, mod3, norm_g, wg, wu, wd]
    if final:
        fg, fmod3 = final_args
        in_specs += [
            pl.BlockSpec((1, d), lambda i: (0, 0)),
            pl.BlockSpec((None, 1, d), lambda i: (i * tm // seq, 0, 0)),
            pl.BlockSpec((None, 1, d), lambda i: (i * tm // seq, 0, 1)),
        ]
        args += [fg, fmod3, fmod3]
    return pl.pallas_call(
        functools.partial(_ffn_kernel, chunks=chunks, final=final),
        grid=(n // tm,),
        in_specs=in_specs,
        out_specs=row_spec,
        out_shape=jax.ShapeDtypeStruct((n, d), F32),
        compiler_params=_compiler_params(1),
        name="ffn_final" if final else "ffn",
    )(*args)


def _mix_in_kernel(x_ref, sh_ref, sc_ref, ng_ref, w_ref, o_ref, *, dc):
    x = x_ref[...]
    h = _norm_modulate(x, ng_ref[...], sh_ref[...], sc_ref[...]).astype(BF16)
    uf = jnp.dot(h, w_ref[:, 0:2 * dc], preferred_element_type=F32)
    o_ref[:, 0:2 * dc] = uf.astype(BF16)
    us = jnp.dot(h, w_ref[:, 2 * dc:5 * dc], preferred_element_type=F32)
    o_ref[:, 2 * dc:3 * dc] = us[:, 0:dc].astype(BF16)
    o_ref[:, 3 * dc:4 * dc] = (us[:, dc:2 * dc] * us[:, 2 * dc:3 * dc]).astype(BF16)
    ug = jnp.dot(h, w_ref[:, 5 * dc:7 * dc], preferred_element_type=F32)
    o_ref[:, 4 * dc:5 * dc] = (ug[:, 0:dc] * _sigmoid(ug[:, dc:2 * dc])).astype(BF16)


def _mix_in_call(x2, mod3, norm_g, w, seq, tm, dc):
    n, d = x2.shape
    return pl.pallas_call(
        functools.partial(_mix_in_kernel, dc=dc),
        grid=(n // tm,),
        in_specs=[
            pl.BlockSpec((tm, d), lambda i: (i, 0)),
            pl.BlockSpec((None, 1, d), lambda i: (i * tm // seq, 0, 3)),
            pl.BlockSpec((None, 1, d), lambda i: (i * tm // seq, 0, 4)),
            pl.BlockSpec((1, d), lambda i: (0, 0)),
            _resident(w.shape, lambda i: (0, 0)),
        ],
        out_specs=pl.BlockSpec((tm, 5 * dc), lambda i: (i, 0)),
        out_shape=jax.ShapeDtypeStruct((n, 5 * dc), BF16),
        compiler_params=_compiler_params(1),
        name="mix_in",
    )(x2, mod3, mod3, norm_g, w)


def _fill_padded(dst_ref, src_ref, t0, tr, seq):
    pad = CONV_PAD_ROWS
    dst_ref[pad:pad + tr, :] = src_ref[pl.ds(t0, tr), :].astype(F32)
    top = src_ref[pl.ds(pl.multiple_of(jnp.maximum(t0 - pad, 0), pad), pad), :].astype(F32)
    dst_ref[0:pad, :] = jnp.where(t0 > 0, top, 0.0)
    bot = src_ref[pl.ds(pl.multiple_of(jnp.minimum(t0 + tr, seq - pad), pad), pad), :].astype(F32)
    dst_ref[pad + tr:pad + tr + pad, :] = jnp.where(t0 + tr < seq, bot, 0.0)


def _depthwise_rows(pad_ref, w_ref, base, rows, width):
    half = (width - 1) // 2
    win_rows = rows + 2 * CONV_PAD_ROWS
    blocks = []
    for cb in range(pad_ref.shape[1] // LANES):
        lanes = slice(cb * LANES, (cb + 1) * LANES)
        win = pad_ref[pl.ds(base, win_rows), lanes]
        acc = None
        for r in range(SUBLANES):
            offs = [o for o in range(CONV_PAD_ROWS - half, CONV_PAD_ROWS + half + 1)
                    if o % SUBLANES == r]
            if not offs:
                continue
            shifted = win if r == 0 else pltpu.roll(win, win_rows - r, 0)
            for o in offs:
                tap = o - (CONV_PAD_ROWS - half)
                term = shifted[o - r:o - r + rows, :] * w_ref[tap:tap + 1, lanes]
                acc = term if acc is None else acc + term
        blocks.append(acc)
    return jnp.concatenate(blocks, axis=-1)


def _mix_seq_kernel(uf_ref, bg_ref, p_ref, cm_ref, sm_ref, w3_ref, o_ref, ppad_ref,
                    *, tr, seq, dc, rows):
    t0 = pl.multiple_of(pl.program_id(1) * tr, tr)
    zf = jnp.dot(cm_ref[...], uf_ref[:, 0:dc], preferred_element_type=F32)
    zf = zf + jnp.dot(sm_ref[...], uf_ref[:, dc:2 * dc], preferred_element_type=F32)
    o_ref[:, 0:dc] = zf.astype(BF16)

    _fill_padded(ppad_ref, p_ref, t0, tr, seq)
    for base in range(0, tr, rows):
        cs = _depthwise_rows(ppad_ref, w3_ref, base, rows, w3_ref.shape[0])
        zs = bg_ref[base:base + rows, :].astype(F32) * cs
        o_ref[base:base + rows, dc:2 * dc] = zs.astype(BF16)


def _mix_seq_call(u3, cmat, smat, w3, tr, dc, rows):
    bsz, seq, _ = u3.shape
    return pl.pallas_call(
        functools.partial(_mix_seq_kernel, tr=tr, seq=seq, dc=dc, rows=rows),
        grid=(bsz, seq // tr),
        in_specs=[
            pl.BlockSpec((None, seq, 2 * dc), lambda b, k: (b, 0, 0)),
            pl.BlockSpec((None, tr, dc), lambda b, k: (b, k, 2)),
            pl.BlockSpec((None, seq, dc), lambda b, k: (b, 0, 3)),
            pl.BlockSpec((tr, seq), lambda b, k: (k, 0)),
            pl.BlockSpec((tr, seq), lambda b, k: (k, 0)),
            pl.BlockSpec(w3.shape, lambda b, k: (0, 0)),
        ],
        out_specs=pl.BlockSpec((None, tr, 2 * dc), lambda b, k: (b, k, 0)),
        out_shape=jax.ShapeDtypeStruct((bsz, seq, 2 * dc), BF16),
        scratch_shapes=[pltpu.VMEM((tr + 2 * CONV_PAD_ROWS, dc), F32)],
        compiler_params=_compiler_params(2),
        name="mix_seq",
    )(u3, u3, u3, cmat, smat, w3)


def _mix_out_kernel(x_ref, z_ref, v_ref, sh_ref, sc_ref, gt_ref, ng_ref, w31_ref, cb_ref,
                    lg_ref, lb_ref, wgate_ref, bgate_ref, wbf_ref, wbs_ref, wbc_ref, wo_ref,
                    o_ref, vpad_ref, *, tm, seq, dc, rows):
    t0 = pl.multiple_of((pl.program_id(0) * tm) % seq, tm)
    x = x_ref[...]
    d = x.shape[1]
    h = _norm_modulate(x, ng_ref[...], sh_ref[...], sc_ref[...]).astype(BF16)

    _fill_padded(vpad_ref, v_ref, t0, tm, seq)

    def conformer_rows(base):
        cv = _depthwise_rows(vpad_ref, w31_ref, base, rows, w31_ref.shape[0]) + cb_ref[...]
        mu = jnp.mean(cv, axis=-1, keepdims=True)
        cen = cv - mu
        var = jnp.mean(cen * cen, axis=-1, keepdims=True)
        ln = cen * lax.rsqrt(var + EPS) * lg_ref[...] + lb_ref[...]
        return (ln * _sigmoid(ln)).astype(BF16)

    zc = jnp.concatenate([conformer_rows(base) for base in range(0, tm, rows)], axis=0)

    merged = None
    for i, wb_ref in enumerate((wbf_ref, wbs_ref, wbc_ref)):
        logits = jnp.dot(h, wgate_ref[:, i * d:(i + 1) * d], preferred_element_type=F32)
        gate = _sigmoid(logits + bgate_ref[:, i * d:(i + 1) * d])
        z = zc if i == 2 else z_ref[:, i * dc:(i + 1) * dc]
        y = jnp.dot(z, wb_ref[...], preferred_element_type=F32)
        merged = gate * y if merged is None else merged + gate * y
    out = jnp.dot(merged.astype(BF16), wo_ref[...], preferred_element_type=F32)
    o_ref[...] = x + gt_ref[...] * out


def _mix_out_call(x2, z2, u3, mod3, norm_g, w31, cb, lg, lb, wgate, bgate, wbf, wbs, wbc, wo,
                  tm, dc, rows):
    n, d = x2.shape
    seq = u3.shape[1]

    def mod_spec(col):
        return pl.BlockSpec((None, 1, d), lambda i: (i * tm // seq, 0, col))

    const = lambda a: _resident(a.shape, lambda i: (0, 0))
    small = lambda a: pl.BlockSpec(a.shape, lambda i: (0, 0))
    return pl.pallas_call(
        functools.partial(_mix_out_kernel, tm=tm, seq=seq, dc=dc, rows=rows),
        grid=(n // tm,),
        in_specs=[
            pl.BlockSpec((tm, d), lambda i: (i, 0)),
            pl.BlockSpec((tm, 2 * dc), lambda i: (i, 0)),
            pl.BlockSpec((None, seq, dc), lambda i: (i * tm // seq, 0, 4)),
            mod_spec(3), mod_spec(4), mod_spec(5),
            small(norm_g), small(w31), small(cb), small(lg), small(lb),
            const(wgate), small(bgate), const(wbf), const(wbs), const(wbc), const(wo),
        ],
        out_specs=pl.BlockSpec((tm, d), lambda i: (i, 0)),
        out_shape=jax.ShapeDtypeStruct((n, d), F32),
        scratch_shapes=[pltpu.VMEM((tm + 2 * CONV_PAD_ROWS, dc), F32)],
        compiler_params=_compiler_params(1),
        name="mix_out",
    )(x2, z2, u3, mod3, mod3, mod3, norm_g, w31, cb, lg, lb, wgate, bgate, wbf, wbs, wbc, wo)


@functools.lru_cache(maxsize=None)
def _seq_dft_matrices(seq):
    idx = np.arange(seq, dtype=np.int64)
    ang = (2.0 * np.pi / seq) * ((idx[:, None] * idx[None, :]) % seq).astype(np.float64)
    return np.cos(ang).astype(np.float32), (-np.sin(ang)).astype(np.float32)


@functools.lru_cache(maxsize=None)
def _chan_dft_matrix(d_f, group, seq):
    idx = np.arange(group, dtype=np.int64)
    ang = (2.0 * np.pi / group) * ((idx[:, None] * idx[None, :]) % group).astype(np.float64)
    scale = 1.0 / np.sqrt(float(seq) * float(group))
    out = np.zeros((d_f, 2 * d_f), np.float64)
    for g in range(d_f // group):
        sl = slice(g * group, (g + 1) * group)
        out[sl, sl] = np.cos(ang) * scale
        out[sl, d_f + g * group:d_f + (g + 1) * group] = np.sin(ang) * scale
    return out.astype(np.float32)


def kernel(x, c, ffn1_norm_g, ffn1_w_gate, ffn1_w_up, ffn1_w_down, mix_norm_g, w_in, conv_short_w, conv_conf_w, conv_conf_b, conf_ln_g, conf_ln_b, w_branch_f, w_branch_s, w_branch_c, w_gate, b_gate, w_out, ffn2_norm_g, ffn2_w_gate, ffn2_w_up, ffn2_w_down, w_mod, b_mod, final_norm_g, w_final_mod, b_final_mod):
    bsz, seq, d = x.shape
    depth = w_mod.shape[0]
    dc = w_branch_f.shape[1]
    n = bsz * seq
    tm_ffn, ffn_chunks = 512, (1536, 1280)
    tm_mix = 512
    tr, conv_rows = 512, 64

    mods = _mod_call(c, w_mod, b_mod.reshape(depth, 1, N_MOD * d))
    fmod = _mod_call(c, w_final_mod[None], b_final_mod.reshape(1, 1, 2 * d))
    fmod3 = fmod.reshape(bsz, 1, 2 * d)

    cmat, smat = _seq_dft_matrices(seq)
    cmat, smat = jnp.asarray(cmat).astype(BF16), jnp.asarray(smat).astype(BF16)
    folded = _fold_call(w_in, jnp.asarray(_chan_dft_matrix(dc, FOURIER_GROUP, seq)), dc)
    w_in_all = jnp.concatenate([folded, w_in[:, :, dc:]], axis=-1).astype(BF16)

    to_bf16 = lambda a: a.astype(BF16)
    f1g, f1u, f1d = to_bf16(ffn1_w_gate), to_bf16(ffn1_w_up), to_bf16(ffn1_w_down)
    f2g, f2u, f2d = to_bf16(ffn2_w_gate), to_bf16(ffn2_w_up), to_bf16(ffn2_w_down)
    wgate_b, wo_b = to_bf16(w_gate), to_bf16(w_out)
    wbf, wbs, wbc = to_bf16(w_branch_f), to_bf16(w_branch_s), to_bf16(w_branch_c)

    x2 = x.reshape(n, d)
    for l in range(depth):
        mod3 = mods[l].reshape(bsz, 1, N_MOD * d)
        x2 = _ffn_call(x2, mod3, 0, ffn1_norm_g[l][None], f1g[l], f1u[l], f1d[l], seq, tm_ffn,
                       ffn_chunks)
        u2 = _mix_in_call(x2, mod3, mix_norm_g[l][None], w_in_all[l], seq, tm_mix, dc)
        u3 = u2.reshape(bsz, seq, 5 * dc)
        z3 = _mix_seq_call(u3, cmat, smat, conv_short_w[l], tr, dc, conv_rows)
        x2 = _mix_out_call(x2, z3.reshape(n, 2 * dc), u3, mod3, mix_norm_g[l][None],
                           conv_conf_w[l], conv_conf_b[l][None], conf_ln_g[l][None],
                           conf_ln_b[l][None], wgate_b[l], b_gate[l][None], wbf[l], wbs[l],
                           wbc[l], wo_b[l], tm_mix, dc, conv_rows)
        final_args = (final_norm_g[None], fmod3) if l == depth - 1 else None
        x2 = _ffn_call(x2, mod3, 2, ffn2_norm_g[l][None], f2g[l], f2u[l], f2d[l], seq, tm_ffn,
                       ffn_chunks, final_args=final_args)
    return x2.reshape(bsz, seq, d)
```
